```python
import math
import jax, jax.numpy as jnp
from jax import lax
import numpy as np

D_MODEL = 1024
BATCH = 8
SEQ = 2048
DEPTH = 2
DEC_BATCH = 32
DEC_SEQ = 1
PAST_LEN = 8192
PAGE_SIZE = 128

N_AB = (DEPTH + 1) // 2
N_C = DEPTH // 2
SB_HEADS = 8
SB_HEAD_DIM = D_MODEL // 16
SB_WIDTH = SB_HEADS * SB_HEAD_DIM
SB_BIAS_INIT = -7.0
POOL_WIDTH = D_MODEL // 2
POOL_WINDOWS = (2, 4, 8, 16)
POOL_GROUPS = len(POOL_WINDOWS)
POOL_GROUP_DIM = POOL_WIDTH // POOL_GROUPS
POOL_STATE = max(POOL_WINDOWS) - 1
CONV_WIDTH = D_MODEL
CONV_KERNEL = 31
CONV_STATE = CONV_KERNEL - 1
Q_BLOCK = 128
LN_EPS = 1e-5
ALPHA = (2.0 * DEPTH) ** 0.25
BETA_INIT = (8.0 * DEPTH) ** -0.25
AB_IN = 4 * SB_WIDTH + 2 * POOL_WIDTH
C_IN = 3 * CONV_WIDTH

kernel_name = "stickbreak_pool_conformer_hybrid_step"


def layer_norm(x, g, b):
    xf = x.astype(jnp.float32)
    mu = jnp.mean(xf, axis=-1, keepdims=True)
    var = jnp.mean(jnp.square(xf - mu), axis=-1, keepdims=True)
    y = (xf - mu) * lax.rsqrt(var + LN_EPS)
    return (y * g.astype(jnp.float32) + b.astype(jnp.float32)).astype(x.dtype)


def stick_breaking(q, k, v, q_pos, k_pos, sb_bias):
    z = jnp.einsum('bqhd,bkhd->bhqk', q, k).astype(jnp.float32) / math.sqrt(SB_HEAD_DIM)
    z = z + sb_bias.astype(jnp.float32)[None, :, None, None]
    causal = k_pos[None, :] < q_pos[:, None]
    log_keep = jnp.where(causal, jax.nn.log_sigmoid(-z), 0.0)
    after = lax.cumsum(log_keep, axis=3, reverse=True) - log_keep
    w = jnp.where(causal, jnp.exp(jax.nn.log_sigmoid(z) + after), 0.0)
    return jnp.einsum('bhqk,bkhd->bqhd', w.astype(v.dtype), v)


def stick_breaking_prompt(q, k, v, sb_bias):
    B, S, H, Dh = q.shape
    nb = S // Q_BLOCK
    qb = q.reshape(B, nb, Q_BLOCK, H, Dh).transpose(1, 0, 2, 3, 4)
    k_pos = jnp.arange(S)

    def block(args):
        qi, bi = args
        q_pos = bi * Q_BLOCK + jnp.arange(Q_BLOCK)
        return stick_breaking(qi, k, v, q_pos, k_pos, sb_bias)

    out = lax.map(block, (qb, jnp.arange(nb)))
    return out.transpose(1, 0, 2, 3, 4).reshape(B, S, H, Dh)


def multiscale_pool(u_ext, start_pos, w_pool, pool_scale):
    B, L, _ = u_ext.shape
    T = L - POOL_STATE
    uf = u_ext.astype(jnp.float32)
    cs = jnp.concatenate([jnp.zeros((B, 1, POOL_WIDTH), jnp.float32), jnp.cumsum(uf, axis=1)], axis=1)
    pos = start_pos + jnp.arange(T)
    diffs = []
    for g, w in enumerate(POOL_WINDOWS):
        c0, c1 = g * POOL_GROUP_DIM, (g + 1) * POOL_GROUP_DIM
        hi = cs[:, POOL_STATE + 1:POOL_STATE + 1 + T, c0:c1]
        lo = cs[:, POOL_STATE + 1 - w:POOL_STATE + 1 - w + T, c0:c1]
        count = jnp.minimum(w, pos + 1).astype(jnp.float32)
        diffs.append((hi - lo) / count[None, :, None] - uf[:, POOL_STATE:, c0:c1])
    d = jnp.stack(diffs, axis=2)
    y = jnp.einsum('btgc,gcd->btgd', d.astype(w_pool.dtype), w_pool) * pool_scale
    return y.reshape(B, T, POOL_WIDTH).astype(u_ext.dtype)


def ab_mix(x, pool_prev, kv_past, start_pos, w_in, sb_bias, w_pool, pool_scale, w_out):
    B, T, _ = x.shape
    p = x @ w_in
    q, k, v, g_a, u_b, g_b = jnp.split(p, [SB_WIDTH, 2 * SB_WIDTH, 3 * SB_WIDTH, 4 * SB_WIDTH, 4 * SB_WIDTH + POOL_WIDTH], axis=-1)
    q = q.reshape(B, T, SB_HEADS, SB_HEAD_DIM)
    k = k.reshape(B, T, SB_HEADS, SB_HEAD_DIM)
    v = v.reshape(B, T, SB_HEADS, SB_HEAD_DIM)
    if kv_past is None:
        o_a = stick_breaking_prompt(q, k, v, sb_bias)
    else:
        k_past, v_past = kv_past
        n_past = k_past.shape[1]
        kk = jnp.concatenate([k_past.astype(k.dtype), k], axis=1)
        vv = jnp.concatenate([v_past.astype(v.dtype), v], axis=1)
        o_a = stick_breaking(q, kk, vv, n_past + jnp.arange(T), jnp.arange(n_past + T), sb_bias)
    u_ext = jnp.concatenate([pool_prev.astype(u_b.dtype), u_b], axis=1)
    o_b = multiscale_pool(u_ext, start_pos, w_pool, pool_scale)
    h = jnp.concatenate([o_a.reshape(B, T, SB_WIDTH) * jax.nn.silu(g_a), o_b * jax.nn.silu(g_b)], axis=-1)
    return h @ w_out, k, v, u_ext[:, -POOL_STATE:]


def c_mix(x, conv_prev, w_in, w_dw, b_dw, cn_g, cn_b, w_out):
    p = x @ w_in
    a, a_gate, gate = jnp.split(p, [CONV_WIDTH, 2 * CONV_WIDTH], axis=-1)
    h = a * jax.nn.sigmoid(a_gate)
    h_ext = jnp.concatenate([conv_prev.astype(h.dtype), h], axis=1)
    c = lax.conv_general_dilated(h_ext, w_dw[:, None, :].astype(h.dtype), window_strides=(1,), padding='VALID',
                                 dimension_numbers=('NWC', 'WIO', 'NWC'), feature_group_count=CONV_WIDTH)
    c = jax.nn.silu(layer_norm(c + b_dw, cn_g, cn_b))
    return (c * jax.nn.silu(gate)) @ w_out, h_ext[:, -CONV_STATE:]


def setup_inputs(seed: int = 0) -> dict:
    key = jax.random.key(seed)
    ks = jax.random.split(key, 24)
    n_pages = PAST_LEN // PAGE_SIZE
    n_pool = (DEC_BATCH * n_pages * 5) // 4
    nrm = lambda k, s: jax.random.normal(k, s, jnp.float32)
    page_table = jax.random.permutation(ks[0], n_pool)[:DEC_BATCH * n_pages].reshape(DEC_BATCH, n_pages).astype(jnp.int32)
    return {
        "x_prompt": nrm(ks[1], (BATCH, SEQ, D_MODEL)),
        "x_sample": nrm(ks[2], (DEC_BATCH, DEC_SEQ, D_MODEL)),
        "cache_k": nrm(ks[3], (N_AB, n_pool, PAGE_SIZE, SB_HEADS, SB_HEAD_DIM)),
        "cache_v": nrm(ks[4], (N_AB, n_pool, PAGE_SIZE, SB_HEADS, SB_HEAD_DIM)),
        "state_pool": nrm(ks[5], (N_AB, DEC_BATCH, POOL_STATE, POOL_WIDTH)),
        "state_conv": 0.5 * nrm(ks[6], (N_C, DEC_BATCH, CONV_STATE, CONV_WIDTH)),
        "page_table": page_table,
        "w_in_ab": nrm(ks[7], (N_AB, D_MODEL, AB_IN)) * D_MODEL ** -0.5,
        "sb_bias": SB_BIAS_INIT + 0.1 * nrm(ks[21], (N_AB, SB_HEADS)),
        "w_pool": nrm(ks[8], (N_AB, POOL_GROUPS, POOL_GROUP_DIM, POOL_GROUP_DIM)) * POOL_GROUP_DIM ** -0.5,
        "pool_scale": 1.0 + 0.1 * nrm(ks[9], (N_AB, POOL_GROUPS, POOL_GROUP_DIM)),
        "w_out_ab": nrm(ks[10], (N_AB, SB_WIDTH + POOL_WIDTH, D_MODEL)) * (SB_WIDTH + POOL_WIDTH) ** -0.5 * BETA_INIT,
        "ln_ab_g": 1.0 + 0.05 * nrm(ks[11], (N_AB, D_MODEL)),
        "ln_ab_b": 0.02 * nrm(ks[12], (N_AB, D_MODEL)),
        "w_in_c": nrm(ks[13], (N_C, D_MODEL, C_IN)) * D_MODEL ** -0.5,
        "w_dw": nrm(ks[14], (N_C, CONV_KERNEL, CONV_WIDTH)) * CONV_KERNEL ** -0.5,
        "b_dw": 0.02 * nrm(ks[15], (N_C, CONV_WIDTH)),
        "conv_norm_g": 1.0 + 0.05 * nrm(ks[16], (N_C, CONV_WIDTH)),
        "conv_norm_b": 0.02 * nrm(ks[17], (N_C, CONV_WIDTH)),
        "w_out_c": nrm(ks[18], (N_C, CONV_WIDTH, D_MODEL)) * CONV_WIDTH ** -0.5 * BETA_INIT,
        "ln_c_g": 1.0 + 0.05 * nrm(ks[19], (N_C, D_MODEL)),
        "ln_c_b": 0.02 * nrm(ks[20], (N_C, D_MODEL)),
    }


def reference(x_prompt, x_sample, cache_k, cache_v, state_pool, state_conv, page_table,
              w_in_ab, sb_bias, w_pool, pool_scale, w_out_ab, ln_ab_g, ln_ab_b,
              w_in_c, w_dw, b_dw, conv_norm_g, conv_norm_b, w_out_c, ln_c_g, ln_c_b):
    xp, xs = x_prompt, x_sample
    bp, bs = xp.shape[0], xs.shape[0]
    kp_l, vp_l, ks_l, vs_l, pp_l, ps_l, cp_l, cs_l = [], [], [], [], [], [], [], []
    for layer in range(DEPTH):
        i = layer // 2
        if layer % 2 == 0:
            prm = (w_in_ab[i], sb_bias[i], w_pool[i], pool_scale[i], w_out_ab[i])
            zero_pool = jnp.zeros((bp, POOL_STATE, POOL_WIDTH), xp.dtype)
            op, kp, vp, pp = ab_mix(xp, zero_pool, None, 0, *prm)
            k_past = cache_k[i][page_table].reshape(bs, PAST_LEN, SB_HEADS, SB_HEAD_DIM)
            v_past = cache_v[i][page_table].reshape(bs, PAST_LEN, SB_HEADS, SB_HEAD_DIM)
            os_, ksn, vsn, psn = ab_mix(xs, state_pool[i], (k_past, v_past), PAST_LEN, *prm)
            xp = layer_norm(ALPHA * xp + op, ln_ab_g[i], ln_ab_b[i])
            xs = layer_norm(ALPHA * xs + os_, ln_ab_g[i], ln_ab_b[i])
            kp_l.append(kp); vp_l.append(vp); ks_l.append(ksn); vs_l.append(vsn)
            pp_l.append(pp); ps_l.append(psn)
        else:
            prm = (w_in_c[i], w_dw[i], b_dw[i], conv_norm_g[i], conv_norm_b[i], w_out_c[i])
            zero_conv = jnp.zeros((bp, CONV_STATE, CONV_WIDTH), xp.dtype)
            op, cp = c_mix(xp, zero_conv, *prm)
            os_, csn = c_mix(xs, state_conv[i], *prm)
            xp = layer_norm(ALPHA * xp + op, ln_c_g[i], ln_c_b[i])
            xs = layer_norm(ALPHA * xs + os_, ln_c_g[i], ln_c_b[i])
            cp_l.append(cp); cs_l.append(csn)
    return (xp, xs, jnp.stack(kp_l), jnp.stack(vp_l), jnp.stack(ks_l), jnp.stack(vs_l),
            jnp.stack(pp_l), jnp.stack(ps_l), jnp.stack(cp_l), jnp.stack(cs_l))
```

```python
import functools
import math

import jax
import jax.numpy as jnp
from jax import lax
from jax.experimental import pallas as pl
from jax.experimental.pallas import tpu as pltpu

F32 = jnp.float32
BF16 = jnp.bfloat16

D_MODEL = 1024
DEPTH = 2
SB_HEADS = 8
SB_HEAD_DIM = 64
SB_WIDTH = SB_HEADS * SB_HEAD_DIM
POOL_WIDTH = 512
POOL_WINDOWS = (2, 4, 8, 16)
POOL_GROUP_DIM = 128
POOL_STATE = 15
CONV_WIDTH = 1024
CONV_KERNEL = 31
CONV_STATE = CONV_KERNEL - 1
PAGE_SIZE = 128
LN_EPS = 1e-5
ALPHA = (2.0 * DEPTH) ** 0.25
Q_SCALE = 1.0 / math.sqrt(SB_HEAD_DIM)

LANES = 128
SUBLANES = 8
HEADS_PER_LANE_TILE = LANES // SB_HEAD_DIM
VMEM_LIMIT = 56 * 1024 * 1024

INPROJ_ROWS = 512
ATTN_TILE = 256
MIX_ROWS = 256
CONV_ROW_CHUNK = 64
POOL_HALO = 16
CONV_HALO = 32
DECODE_PAGES = 8
CUMSUM_CHUNK = 256


def _sigmoid(x):
    return 1.0 / (1.0 + jnp.exp(-x))


def _silu(x):
    return x * _sigmoid(x)


def _softplus(z):
    return jnp.maximum(z, 0.0) + jnp.log(1.0 + jnp.exp(-jnp.abs(z)))


def _layer_norm(x, g, b):
    mu = jnp.mean(x, axis=-1, keepdims=True)
    xc = x - mu
    var = jnp.mean(xc * xc, axis=-1, keepdims=True)
    return xc * lax.rsqrt(var + LN_EPS) * g + b


def _dot(a, b):
    return jnp.dot(a, b, preferred_element_type=F32)


def _dot_nt(a, b):
    return lax.dot_general(a, b, (((1,), (1,)), ((), ())), preferred_element_type=F32)


def _split_bf16(x):
    hi = x.astype(BF16)
    lo = (x - hi.astype(F32)).astype(BF16)
    return hi, lo


def _strict_lower_ones(n):
    r = lax.broadcasted_iota(jnp.int32, (n, n), 0)
    c = lax.broadcasted_iota(jnp.int32, (n, n), 1)
    return jnp.where(r > c, 1.0, 0.0).astype(BF16)


def _const_spec(shape):
    zeros = (0,) * len(shape)
    return pl.BlockSpec(shape, lambda *_: zeros)


def _inproj_ab_kernel(x_ref, w_ref, q_ref, k_ref, v_ref, kb_ref, vb_ref, ga_ref, ub_ref, gb_ref):
    xb = x_ref[...].astype(BF16)

    def proj(c):
        return _dot(xb, w_ref[:, c * SB_WIDTH:(c + 1) * SB_WIDTH])

    q_ref[...] = (proj(0) * Q_SCALE).astype(BF16)
    k = proj(1)
    k_ref[...] = k
    kb_ref[...] = k.astype(BF16)
    v = proj(2)
    v_ref[...] = v
    vb_ref[...] = v.astype(BF16)
    ga_ref[...] = proj(3)
    ub_ref[...] = proj(4)
    gb_ref[...] = proj(5)


def _inproj_ab(x2d, w_bf, rows):
    m = x2d.shape[0]
    col = lambda i: (i, 0)
    wide = lambda dt: jax.ShapeDtypeStruct((m, SB_WIDTH), dt)
    out_block = pl.BlockSpec((rows, SB_WIDTH), col)
    return pl.pallas_call(
        _inproj_ab_kernel,
        grid=(m // rows,),
        in_specs=[pl.BlockSpec((rows, D_MODEL), col), _const_spec(w_bf.shape)],
        out_specs=[out_block] * 8,
        out_shape=[wide(BF16), wide(F32), wide(F32), wide(BF16), wide(BF16), wide(F32), wide(F32), wide(F32)],
        compiler_params=pltpu.CompilerParams(dimension_semantics=("arbitrary",), vmem_limit_bytes=VMEM_LIMIT),
        name="inproj_ab",
    )(x2d, w_bf)


def _sb_tile(qh, kt, vt, bias, upper, o_acc, c_acc, causal):
    z = _dot_nt(qh, kt) + bias
    sp = _softplus(z)
    ls = z - sp
    if causal is not None:
        sp = jnp.where(causal, sp, 0.0)
    hi, lo = _split_bf16(sp)
    after = _dot(hi, upper) + _dot(lo, upper)
    w = jnp.exp(ls - after - c_acc)
    if causal is not None:
        w = jnp.where(causal, w, 0.0)
    o_acc = o_acc + _dot(w.astype(BF16), vt)
    c_acc = c_acc + after[:, :1] + sp[:, :1]
    return o_acc, c_acc


def _attn_kernel(bias_ref, q_ref, k_ref, v_ref, g_ref, o_ref, *, tile):
    pair = pl.program_id(1)
    qi = pl.program_id(2)
    lane = lax.broadcasted_iota(jnp.int32, (1, LANES), 1)
    q = q_ref[...].astype(F32)
    q_heads = [jnp.where((lane // SB_HEAD_DIM) == h, q, 0.0).astype(BF16) for h in range(HEADS_PER_LANE_TILE)]
    biases = [bias_ref[HEADS_PER_LANE_TILE * pair + h] for h in range(HEADS_PER_LANE_TILE)]
    upper = _strict_lower_ones(tile)
    row = lax.broadcasted_iota(jnp.int32, (tile, tile), 0)
    col = lax.broadcasted_iota(jnp.int32, (tile, tile), 1)
    causal = col < row

    def visit(start, carry, mask):
        kt = k_ref[pl.ds(start, tile), :]
        vt = v_ref[pl.ds(start, tile), :]
        return tuple(_sb_tile(q_heads[h], kt, vt, biases[h], upper, carry[h][0], carry[h][1], mask)
                     for h in range(HEADS_PER_LANE_TILE))

    init = tuple((jnp.zeros((tile, LANES), F32), jnp.zeros((tile, 1), F32)) for _ in range(HEADS_PER_LANE_TILE))
    carry = visit(pl.multiple_of(qi * tile, tile), init, causal)

    def body(step, carry):
        start = pl.multiple_of((qi - 1 - step) * tile, tile)
        return visit(start, carry, None)

    carry = lax.fori_loop(0, qi, body, carry)
    o = jnp.where((lane // SB_HEAD_DIM) == 0, carry[0][0], carry[1][0])
    o_ref[...] = (o * _silu(g_ref[...])).astype(BF16)


def _prompt_attention(sb_bias, q_bf, k_bf, v_bf, g_a, batch, seq):
    tile = ATTN_TILE
    nq = seq // tile
    pairs = SB_WIDTH // LANES
    qmap = lambda b, p, i: (b * nq + i, p)
    kvmap = lambda b, p, i: (b, p)
    return pl.pallas_call(
        functools.partial(_attn_kernel, tile=tile),
        grid=(batch, pairs, nq),
        in_specs=[
            pl.BlockSpec(memory_space=pltpu.SMEM),
            pl.BlockSpec((tile, LANES), qmap),
            pl.BlockSpec((seq, LANES), kvmap),
            pl.BlockSpec((seq, LANES), kvmap),
            pl.BlockSpec((tile, LANES), qmap),
        ],
        out_specs=pl.BlockSpec((tile, LANES), qmap),
        out_shape=jax.ShapeDtypeStruct((batch * seq, SB_WIDTH), BF16),
        compiler_params=pltpu.CompilerParams(
            dimension_semantics=("arbitrary", "arbitrary", "arbitrary"), vmem_limit_bytes=VMEM_LIMIT),
        name="prompt_attention",
    )(sb_bias, q_bf, k_bf, v_bf, g_a)


def _pool_branch(window_sum, token, count, wp_ref, ps_ref, group):
    c0 = group * POOL_GROUP_DIM
    d = window_sum / count - token
    return _dot(d.astype(BF16), wp_ref[group]) * ps_ref[:, c0:c0 + POOL_GROUP_DIM]


def _ab_out(x, h_a_bf, o_b, g_b, wo_ref, g_ref, b_ref):
    h_b = (o_b * _silu(g_b)).astype(BF16)
    op = _dot(h_a_bf, wo_ref[0:SB_WIDTH, :]) + _dot(h_b, wo_ref[SB_WIDTH:SB_WIDTH + POOL_WIDTH, :])
    return _layer_norm(ALPHA * x + op, g_ref[...], b_ref[...])


def _ab_out_prompt_kernel(x_ref, ha_ref, ub_ref, gb_ref, wp_ref, ps_ref, wo_ref, g_ref, b_ref,
                          x1_ref, uext_ref, *, rows):
    t = pl.program_id(1)

    @pl.when(t == 0)
    def _():
        uext_ref[0:POOL_HALO, :] = jnp.zeros((POOL_HALO, POOL_WIDTH), F32)

    ub = ub_ref[...]
    uext_ref[POOL_HALO:POOL_HALO + rows, :] = ub
    pos = lax.broadcasted_iota(jnp.int32, (rows, 1), 0) + t * rows
    parts = []
    for g, w in enumerate(POOL_WINDOWS):
        c0 = g * POOL_GROUP_DIM
        token = ub[:, c0:c0 + POOL_GROUP_DIM]
        acc = token
        for d in range(1, w):
            acc = acc + uext_ref[POOL_HALO - d:POOL_HALO - d + rows, c0:c0 + POOL_GROUP_DIM]
        count = jnp.minimum(w, pos + 1).astype(F32)
        parts.append(_pool_branch(acc, token, count, wp_ref, ps_ref, g))
    o_b = jnp.concatenate(parts, axis=-1)
    x1_ref[...] = _ab_out(x_ref[...], ha_ref[...], o_b, gb_ref[...], wo_ref, g_ref, b_ref)
    uext_ref[0:POOL_HALO, :] = uext_ref[rows:rows + POOL_HALO, :]


def _ab_out_prompt(x2d, h_a, u_b, g_b, wp_bf, ps, wo_bf, ln_g, ln_b, batch, seq):
    rows = MIX_ROWS
    nt = seq // rows
    rmap = lambda b, t: (b * nt + t, 0)
    return pl.pallas_call(
        functools.partial(_ab_out_prompt_kernel, rows=rows),
        grid=(batch, nt),
        in_specs=[
            pl.BlockSpec((rows, D_MODEL), rmap),
            pl.BlockSpec((rows, SB_WIDTH), rmap),
            pl.BlockSpec((rows, POOL_WIDTH), rmap),
            pl.BlockSpec((rows, POOL_WIDTH), rmap),
            _const_spec(wp_bf.shape), _const_spec(ps.shape), _const_spec(wo_bf.shape),
            _const_spec(ln_g.shape), _const_spec(ln_b.shape),
        ],
        out_specs=pl.BlockSpec((rows, D_MODEL), rmap),
        out_shape=jax.ShapeDtypeStruct((batch * seq, D_MODEL), F32),
        scratch_shapes=[pltpu.VMEM((POOL_HALO + rows, POOL_WIDTH), F32)],
        compiler_params=pltpu.CompilerParams(
            dimension_semantics=("arbitrary", "arbitrary"), vmem_limit_bytes=VMEM_LIMIT),
        name="ab_out_prompt",
    )(x2d, h_a, u_b, g_b, wp_bf, ps, wo_bf, ln_g, ln_b)


def _glu_and_gate(x1, wi_ref):
    xb = x1.astype(BF16)
    a = _dot(xb, wi_ref[:, 0:CONV_WIDTH])
    a_gate = _dot(xb, wi_ref[:, CONV_WIDTH:2 * CONV_WIDTH])
    gate = _dot(xb, wi_ref[:, 2 * CONV_WIDTH:3 * CONV_WIDTH])
    return a * _sigmoid(a_gate), gate


def _c_out(x1, conv, gate, bdw_ref, cg_ref, cb_ref, wo_ref, g_ref, b_ref):
    c = _silu(_layer_norm(conv + bdw_ref[...], cg_ref[...], cb_ref[...]))
    op = _dot((c * _silu(gate)).astype(BF16), wo_ref[...])
    return _layer_norm(ALPHA * x1 + op, g_ref[...], b_ref[...])


def _cmix_prompt_kernel(x1_ref, wi_ref, wdw_ref, bdw_ref, cg_ref, cb_ref, wo_ref, g_ref, b_ref,
                        y_ref, htail_ref, hext_ref, conv_ref, *, rows):
    t = pl.program_id(1)

    @pl.when(t == 0)
    def _():
        hext_ref[0:CONV_HALO, :] = jnp.zeros((CONV_HALO, CONV_WIDTH), F32)

    x1 = x1_ref[...]
    h, gate = _glu_and_gate(x1, wi_ref)
    hext_ref[CONV_HALO:CONV_HALO + rows, :] = h
    base = CONV_HALO - CONV_STATE
    for cb in range(CONV_WIDTH // LANES):
        cols = slice(cb * LANES, (cb + 1) * LANES)
        taps = [wdw_ref[k:k + 1, cols] for k in range(CONV_KERNEL)]
        for rb in range(rows // CONV_ROW_CHUNK):
            r0 = rb * CONV_ROW_CHUNK
            acc = hext_ref[base + r0:base + r0 + CONV_ROW_CHUNK, cols] * taps[0]
            for k in range(1, CONV_KERNEL):
                acc = acc + hext_ref[base + r0 + k:base + r0 + k + CONV_ROW_CHUNK, cols] * taps[k]
            conv_ref[r0:r0 + CONV_ROW_CHUNK, cols] = acc
    y_ref[...] = _c_out(x1, conv_ref[...], gate, bdw_ref, cg_ref, cb_ref, wo_ref, g_ref, b_ref)
    hext_ref[0:CONV_HALO, :] = hext_ref[rows:rows + CONV_HALO, :]
    htail_ref[...] = hext_ref[rows:rows + CONV_HALO, :]


def _cmix_prompt(x1, wi_bf, w_dw, b_dw, cn_g, cn_b, wo_bf, ln_g, ln_b, batch, seq):
    rows = MIX_ROWS
    nt = seq // rows
    rmap = lambda b, t: (b * nt + t, 0)
    consts = [wi_bf, w_dw, b_dw, cn_g, cn_b, wo_bf, ln_g, ln_b]
    return pl.pallas_call(
        functools.partial(_cmix_prompt_kernel, rows=rows),
        grid=(batch, nt),
        in_specs=[pl.BlockSpec((rows, D_MODEL), rmap)] + [_const_spec(c.shape) for c in consts],
        out_specs=[pl.BlockSpec((rows, D_MODEL), rmap),
                   pl.BlockSpec((None, CONV_HALO, CONV_WIDTH), lambda b, t: (b, 0, 0))],
        out_shape=[jax.ShapeDtypeStruct((batch * seq, D_MODEL), F32),
                   jax.ShapeDtypeStruct((batch, CONV_HALO, CONV_WIDTH), F32)],
        scratch_shapes=[pltpu.VMEM((CONV_HALO + rows, CONV_WIDTH), F32),
                        pltpu.VMEM((rows, CONV_WIDTH), F32)],
        compiler_params=pltpu.CompilerParams(
            dimension_semantics=("arbitrary", "arbitrary"), vmem_limit_bytes=VMEM_LIMIT),
        name="cmix_prompt",
    )(x1, *consts)


def _decode_kernel(pt_ref, bias_ref, q_ref, *refs, pages):
    del pt_ref
    k_refs = refs[:pages]
    v_refs = refs[pages:2 * pages]
    o_ref, carry_ref, acc_ref = refs[2 * pages:]
    step = pl.program_id(1)

    @pl.when(step == 0)
    def _():
        carry_ref[...] = jnp.zeros_like(carry_ref)
        acc_ref[...] = jnp.zeros_like(acc_ref)

    head = lax.broadcasted_iota(jnp.int32, (SB_HEADS, SB_WIDTH), 0)
    owner = lax.broadcasted_iota(jnp.int32, (SB_HEADS, SB_WIDTH), 1) // SB_HEAD_DIM
    own = head == owner
    q = jnp.broadcast_to(q_ref[...].astype(F32), (SB_HEADS, SB_WIDTH))
    q_heads = jnp.where(own, q, 0.0).astype(BF16)
    kb = jnp.concatenate([r[...] for r in k_refs], axis=0).astype(BF16)
    vb = jnp.concatenate([r[...] for r in v_refs], axis=0).astype(BF16)
    z = _dot_nt(q_heads, kb) + bias_ref[...]
    sp = _softplus(z)
    ls = z - sp
    nchunk = pages * PAGE_SIZE // CUMSUM_CHUNK
    chunk = lambda x, c: x[:, c * CUMSUM_CHUNK:(c + 1) * CUMSUM_CHUNK]
    sp_stack = jnp.concatenate([chunk(sp, c) for c in range(nchunk)], axis=0)
    hi, lo = _split_bf16(sp_stack)
    after2 = _dot(jnp.concatenate([hi, lo], axis=0), _strict_lower_ones(CUMSUM_CHUNK))
    rows = nchunk * SB_HEADS
    after = after2[0:rows] + after2[rows:2 * rows]
    carry = carry_ref[...]
    w_parts = [None] * nchunk
    for c in reversed(range(nchunk)):
        a_c = after[c * SB_HEADS:(c + 1) * SB_HEADS]
        sp_c = chunk(sp, c)
        w_parts[c] = jnp.exp(chunk(ls, c) - a_c - carry)
        carry = carry + a_c[:, :1] + sp_c[:, :1]
    w = jnp.concatenate(w_parts, axis=1).astype(BF16)
    acc_ref[...] += _dot(w, vb)
    carry_ref[...] = carry

    @pl.when(step == pl.num_programs(1) - 1)
    def _():
        o_ref[...] = jnp.sum(jnp.where(own, acc_ref[...], 0.0), axis=0, keepdims=True)


def _decode_attention(page_table, bias_col, q_bf, cache_k, cache_v):
    nb, n_pages = page_table.shape
    pages = DECODE_PAGES
    steps = n_pages // pages

    def page_map(slot):
        return lambda b, s, pt: (pt[b, n_pages - (s + 1) * pages + slot], 0, 0)

    page_specs = [pl.BlockSpec((None, PAGE_SIZE, SB_WIDTH), page_map(i)) for i in range(pages)]
    qmap = lambda b, s, pt: (b, 0, 0)
    grid_spec = pltpu.PrefetchScalarGridSpec(
        num_scalar_prefetch=1,
        grid=(nb, steps),
        in_specs=[pl.BlockSpec((SB_HEADS, 1), lambda b, s, pt: (0, 0)),
                  pl.BlockSpec((None, 1, SB_WIDTH), qmap)] + page_specs + page_specs,
        out_specs=pl.BlockSpec((None, 1, SB_WIDTH), qmap),
        scratch_shapes=[pltpu.VMEM((SB_HEADS, 1), F32), pltpu.VMEM((SB_HEADS, SB_WIDTH), F32)],
    )
    return pl.pallas_call(
        functools.partial(_decode_kernel, pages=pages),
        grid_spec=grid_spec,
        out_shape=jax.ShapeDtypeStruct((nb, 1, SB_WIDTH), F32),
        compiler_params=pltpu.CompilerParams(
            dimension_semantics=("arbitrary", "arbitrary"), vmem_limit_bytes=VMEM_LIMIT),
        name="decode_attention",
    )(page_table, bias_col, q_bf.reshape(nb, 1, SB_WIDTH), *([cache_k] * pages), *([cache_v] * pages))


def _sample_tail_kernel(x_ref, oa_ref, ga_ref, ub_ref, gb_ref, pool_ref, conv_ref,
                        wp_ref, ps_ref, wo_ab_ref, g_ab_ref, b_ab_ref,
                        wi_ref, wdw_ref, bdw_ref, cg_ref, cb_ref, wo_c_ref, g_c_ref, b_c_ref,
                        y_ref, h_ref):
    ub = ub_ref[...]
    parts = []
    for g, w in enumerate(POOL_WINDOWS):
        c0 = g * POOL_GROUP_DIM
        token = ub[:, c0:c0 + POOL_GROUP_DIM]
        acc = token
        for d in range(1, w):
            acc = acc + pool_ref[POOL_STATE - d, :, c0:c0 + POOL_GROUP_DIM]
        parts.append(_pool_branch(acc, token, float(w), wp_ref, ps_ref, g))
    o_b = jnp.concatenate(parts, axis=-1)
    h_a = (oa_ref[...] * _silu(ga_ref[...])).astype(BF16)
    x1 = _ab_out(x_ref[...], h_a, o_b, gb_ref[...], wo_ab_ref, g_ab_ref, b_ab_ref)

    h, gate = _glu_and_gate(x1, wi_ref)
    h_ref[...] = h
    conv = h * wdw_ref[CONV_STATE:CONV_KERNEL, :]
    for k in range(CONV_STATE):
        conv = conv + conv_ref[k] * wdw_ref[k:k + 1, :]
    y_ref[...] = _c_out(x1, conv, gate, bdw_ref, cg_ref, cb_ref, wo_c_ref, g_c_ref, b_c_ref)


def _sample_tail(*args):
    nb = args[0].shape[0]
    return pl.pallas_call(
        _sample_tail_kernel,
        out_shape=[jax.ShapeDtypeStruct((nb, D_MODEL), F32), jax.ShapeDtypeStruct((nb, CONV_WIDTH), F32)],
        compiler_params=pltpu.CompilerParams(vmem_limit_bytes=VMEM_LIMIT),
        name="sample_tail",
    )(*args)


def kernel(x_prompt, x_sample, cache_k, cache_v, state_pool, state_conv, page_table, w_in_ab, sb_bias, w_pool,
           pool_scale, w_out_ab, ln_ab_g, ln_ab_b, w_in_c, w_dw, b_dw, conv_norm_g, conv_norm_b, w_out_c,
           ln_c_g, ln_c_b):
    assert DEPTH == 2 and w_in_ab.shape[0] == 1 and w_in_c.shape[0] == 1
    bp, seq, _ = x_prompt.shape
    bs, dec_seq, _ = x_sample.shape
    assert dec_seq == 1 and page_table.shape[1] * PAGE_SIZE >= max(POOL_WINDOWS)
    assert seq % ATTN_TILE == 0 and seq % MIX_ROWS == 0 and (bp * seq) % INPROJ_ROWS == 0
    n_pool = cache_k.shape[1]

    w_in_ab_bf = w_in_ab[0].astype(BF16)
    w_out_ab_bf = w_out_ab[0].astype(BF16)
    w_pool_bf = w_pool[0].astype(BF16)
    w_in_c_bf = w_in_c[0].astype(BF16)
    w_out_c_bf = w_out_c[0].astype(BF16)
    ps = pool_scale[0].reshape(1, POOL_WIDTH)
    row = lambda a: a[0].reshape(1, -1)
    ln_ab = (row(ln_ab_g), row(ln_ab_b))
    c_params = (w_in_c_bf, w_dw[0], row(b_dw), row(conv_norm_g), row(conv_norm_b), w_out_c_bf,
                row(ln_c_g), row(ln_c_b))

    xp2d = x_prompt.reshape(bp * seq, D_MODEL)
    q_p, k_p, v_p, kb_p, vb_p, ga_p, ub_p, gb_p = _inproj_ab(xp2d, w_in_ab_bf, INPROJ_ROWS)
    ha_p = _prompt_attention(sb_bias[0], q_p, kb_p, vb_p, ga_p, bp, seq)
    x1_p = _ab_out_prompt(xp2d, ha_p, ub_p, gb_p, w_pool_bf, ps, w_out_ab_bf, *ln_ab, bp, seq)
    y_p, htail_p = _cmix_prompt(x1_p, *c_params, bp, seq)

    xs2d = x_sample.reshape(bs, D_MODEL)
    q_s, k_s, v_s, _, _, ga_s, ub_s, gb_s = _inproj_ab(xs2d, w_in_ab_bf, bs)
    ck = cache_k[0].reshape(n_pool, PAGE_SIZE, SB_WIDTH)
    cv = cache_v[0].reshape(n_pool, PAGE_SIZE, SB_WIDTH)
    oa_s = _decode_attention(page_table, sb_bias[0].reshape(SB_HEADS, 1), q_s, ck, cv).reshape(bs, SB_WIDTH)
    pool_tm = jnp.transpose(state_pool[0], (1, 0, 2))
    conv_tm = jnp.transpose(state_conv[0], (1, 0, 2))
    y_s, h_s = _sample_tail(xs2d, oa_s, ga_s, ub_s, gb_s, pool_tm, conv_tm,
                            w_pool_bf, ps, w_out_ab_bf, *ln_ab, *c_params)

    heads = (SB_HEADS, SB_HEAD_DIM)
    pool_prompt = ub_p.reshape(bp, seq, POOL_WIDTH)[:, seq - POOL_STATE:]
    pool_sample = jnp.concatenate([state_pool[0][:, 1:], ub_s[:, None, :]], axis=1)
    conv_prompt = htail_p[:, CONV_HALO - CONV_STATE:]
    conv_sample = jnp.concatenate([state_conv[0][:, 1:], h_s[:, None, :]], axis=1)
    return (y_p.reshape(bp, seq, D_MODEL), y_s.reshape(bs, 1, D_MODEL),
            k_p.reshape(1, bp, seq, *heads), v_p.reshape(1, bp, seq, *heads),
            k_s.reshape(1, bs, 1, *heads), v_s.reshape(1, bs, 1, *heads),
            pool_prompt[None], pool_sample[None], conv_prompt[None], conv_sample[None])
```

```python
import functools
import math

import jax
import jax.numpy as jnp
from jax import lax
from jax.experimental import pallas as pl
from jax.experimental.pallas import tpu as pltpu

F32 = jnp.float32
BF16 = jnp.bfloat16

D_MODEL = 1024
DEPTH = 2
SB_HEADS = 8
SB_HEAD_DIM = 64
SB_WIDTH = SB_HEADS * SB_HEAD_DIM
POOL_WIDTH = 512
POOL_WINDOWS = (2, 4, 8, 16)
POOL_GROUP_DIM = 128
POOL_STATE = 15
CONV_WIDTH = 1024
CONV_KERNEL = 31
CONV_STATE = CONV_KERNEL - 1
PAGE_SIZE = 128
LN_EPS = 1e-5
ALPHA = (2.0 * DEPTH) ** 0.25
Q_SCALE = 1.0 / math.sqrt(SB_HEAD_DIM)

LANES = 128
SUBLANES = 8
HEADS_PER_LANE_TILE = LANES // SB_HEAD_DIM
VMEM_LIMIT = 56 * 1024 * 1024

INPROJ_ROWS = 512
ATTN_TILE = 256
ATTN_LANE_TILES = 4
MIX_ROWS = 256
CONV_ROW_CHUNK = 64
POOL_HALO = 16
CONV_HALO = 32
DECODE_PAGES = 8


def _sigmoid(x):
    return 1.0 / (1.0 + jnp.exp(-x))


def _silu(x):
    return x * _sigmoid(x)


def _softplus(z):
    return jnp.maximum(z, 0.0) + jnp.log(1.0 + jnp.exp(-jnp.abs(z)))


def _layer_norm(x, g, b):
    mu = jnp.mean(x, axis=-1, keepdims=True)
    xc = x - mu
    var = jnp.mean(xc * xc, axis=-1, keepdims=True)
    return xc * lax.rsqrt(var + LN_EPS) * g + b


def _dot(a, b):
    return jnp.dot(a, b, preferred_element_type=F32)


def _dot_nt(a, b):
    return lax.dot_general(a, b, (((1,), (1,)), ((), ())), preferred_element_type=F32)


def _split_bf16(x):
    hi = x.astype(BF16)
    lo = (x - hi.astype(F32)).astype(BF16)
    return hi, lo


def _strict_lower_ones(n):
    r = lax.broadcasted_iota(jnp.int32, (n, n), 0)
    c = lax.broadcasted_iota(jnp.int32, (n, n), 1)
    return jnp.where(r > c, 1.0, 0.0).astype(BF16)


def _const_spec(shape):
    zeros = (0,) * len(shape)
    return pl.BlockSpec(shape, lambda *_: zeros)


def _inproj_ab_kernel(x_ref, w_ref, q_ref, k_ref, v_ref, kb_ref, vb_ref, ga_ref, ub_ref, gb_ref):
    xb = x_ref[...].astype(BF16)

    def proj(c):
        return _dot(xb, w_ref[:, c * SB_WIDTH:(c + 1) * SB_WIDTH])

    q_ref[...] = (proj(0) * Q_SCALE).astype(BF16)
    k = proj(1)
    k_ref[...] = k
    kb_ref[...] = k.astype(BF16)
    v = proj(2)
    v_ref[...] = v
    vb_ref[...] = v.astype(BF16)
    ga_ref[...] = proj(3)
    ub_ref[...] = proj(4)
    gb_ref[...] = proj(5)


def _inproj_ab(x2d, w_bf, rows):
    m = x2d.shape[0]
    col = lambda i: (i, 0)
    wide = lambda dt: jax.ShapeDtypeStruct((m, SB_WIDTH), dt)
    out_block = pl.BlockSpec((rows, SB_WIDTH), col)
    return pl.pallas_call(
        _inproj_ab_kernel,
        grid=(m // rows,),
        in_specs=[pl.BlockSpec((rows, D_MODEL), col), _const_spec(w_bf.shape)],
        out_specs=[out_block] * 8,
        out_shape=[wide(BF16), wide(F32), wide(F32), wide(BF16), wide(BF16), wide(F32), wide(F32), wide(F32)],
        compiler_params=pltpu.CompilerParams(dimension_semantics=("arbitrary",), vmem_limit_bytes=VMEM_LIMIT),
        name="inproj_ab",
    )(x2d, w_bf)


def _sb_tile(qh, kt, vt, bias, upper, o_acc, c_acc, causal):
    z = _dot_nt(qh, kt) + bias
    sp = _softplus(z)
    ls = z - sp
    if causal is not None:
        sp = jnp.where(causal, sp, 0.0)
    hi, lo = _split_bf16(sp)
    after = _dot(hi, upper) + _dot(lo, upper)
    w = jnp.exp(ls - after - c_acc)
    if causal is not None:
        w = jnp.where(causal, w, 0.0)
    o_acc = o_acc + _dot(w.astype(BF16), vt)
    c_acc = c_acc + after[:, :1] + sp[:, :1]
    return o_acc, c_acc


def _attn_kernel(bias_ref, q_ref, k_ref, v_ref, g_ref, o_ref, *, tile, lane_tiles):
    group = pl.program_id(1)
    qi = pl.program_id(2)
    nheads = lane_tiles * HEADS_PER_LANE_TILE
    lane = lax.broadcasted_iota(jnp.int32, (1, LANES), 1)
    q_heads, biases = [], []
    for h in range(nheads):
        lt, sub = divmod(h, HEADS_PER_LANE_TILE)
        q = q_ref[:, lt * LANES:(lt + 1) * LANES].astype(F32)
        q_heads.append(jnp.where((lane // SB_HEAD_DIM) == sub, q, 0.0).astype(BF16))
        biases.append(bias_ref[nheads * group + h])
    upper = _strict_lower_ones(tile)
    row = lax.broadcasted_iota(jnp.int32, (tile, tile), 0)
    col = lax.broadcasted_iota(jnp.int32, (tile, tile), 1)
    causal = col < row

    def visit(start, carry, mask):
        hs = range(nheads)
        tiles = lambda ref: [ref[pl.ds(start, tile), lt * LANES:(lt + 1) * LANES] for lt in range(lane_tiles)]
        kts, vts = tiles(k_ref), tiles(v_ref)
        z = [_dot_nt(q_heads[h], kts[h // HEADS_PER_LANE_TILE]) + biases[h] for h in hs]
        sp = [_softplus(z[h]) for h in hs]
        ls = [z[h] - sp[h] for h in hs]
        if mask is not None:
            sp = [jnp.where(mask, s, 0.0) for s in sp]
        split = [_split_bf16(s) for s in sp]
        after = [_dot(split[h][0], upper) + _dot(split[h][1], upper) for h in hs]
        w = [jnp.exp(ls[h] - after[h] - carry[h][1]) for h in hs]
        if mask is not None:
            w = [jnp.where(mask, x, 0.0) for x in w]
        o = [carry[h][0] + _dot(w[h].astype(BF16), vts[h // HEADS_PER_LANE_TILE]) for h in hs]
        c = [carry[h][1] + after[h][:, :1] + sp[h][:, :1] for h in hs]
        return tuple((o[h], c[h]) for h in hs)

    init = tuple((jnp.zeros((tile, LANES), F32), jnp.zeros((tile, 1), F32)) for _ in range(nheads))
    carry = visit(pl.multiple_of(qi * tile, tile), init, causal)

    def body(step, carry):
        start = pl.multiple_of((qi - 1 - step) * tile, tile)
        return visit(start, carry, None)

    carry = lax.fori_loop(0, qi, body, carry)
    for lt in range(lane_tiles):
        h0 = lt * HEADS_PER_LANE_TILE
        o = jnp.where((lane // SB_HEAD_DIM) == 0, carry[h0][0], carry[h0 + 1][0])
        cols = slice(lt * LANES, (lt + 1) * LANES)
        o_ref[:, cols] = (o * _silu(g_ref[:, cols])).astype(BF16)


def _prompt_attention(sb_bias, q_bf, k_bf, v_bf, g_a, batch, seq):
    tile = ATTN_TILE
    nq = seq // tile
    width = ATTN_LANE_TILES * LANES
    groups = SB_WIDTH // width
    qmap = lambda b, p, i: (b * nq + i, p)
    kvmap = lambda b, p, i: (b, p)
    return pl.pallas_call(
        functools.partial(_attn_kernel, tile=tile, lane_tiles=ATTN_LANE_TILES),
        grid=(batch, groups, nq),
        in_specs=[
            pl.BlockSpec(memory_space=pltpu.SMEM),
            pl.BlockSpec((tile, width), qmap),
            pl.BlockSpec((seq, width), kvmap),
            pl.BlockSpec((seq, width), kvmap),
            pl.BlockSpec((tile, width), qmap),
        ],
        out_specs=pl.BlockSpec((tile, width), qmap),
        out_shape=jax.ShapeDtypeStruct((batch * seq, SB_WIDTH), BF16),
        compiler_params=pltpu.CompilerParams(
            dimension_semantics=("arbitrary", "arbitrary", "arbitrary"), vmem_limit_bytes=VMEM_LIMIT),
        name="prompt_attention",
    )(sb_bias, q_bf, k_bf, v_bf, g_a)


def _pool_branch(window_sum, token, count, wp_ref, ps_ref, group):
    c0 = group * POOL_GROUP_DIM
    d = window_sum / count - token
    return _dot(d.astype(BF16), wp_ref[group]) * ps_ref[:, c0:c0 + POOL_GROUP_DIM]


def _ab_out(x, h_a_bf, o_b, g_b, wo_ref, g_ref, b_ref):
    h_b = (o_b * _silu(g_b)).astype(BF16)
    op = _dot(h_a_bf, wo_ref[0:SB_WIDTH, :]) + _dot(h_b, wo_ref[SB_WIDTH:SB_WIDTH + POOL_WIDTH, :])
    return _layer_norm(ALPHA * x + op, g_ref[...], b_ref[...])


def _ab_out_prompt_kernel(x_ref, ha_ref, ub_ref, gb_ref, wp_ref, ps_ref, wo_ref, g_ref, b_ref,
                          x1_ref, uext_ref, *, rows):
    t = pl.program_id(1)

    @pl.when(t == 0)
    def _():
        uext_ref[0:POOL_HALO, :] = jnp.zeros((POOL_HALO, POOL_WIDTH), F32)

    ub = ub_ref[...]
    uext_ref[POOL_HALO:POOL_HALO + rows, :] = ub
    pos = lax.broadcasted_iota(jnp.int32, (rows, 1), 0) + t * rows
    parts = []
    for g, w in enumerate(POOL_WINDOWS):
        c0 = g * POOL_GROUP_DIM
        token = ub[:, c0:c0 + POOL_GROUP_DIM]
        acc = token
        for d in range(1, w):
            acc = acc + uext_ref[POOL_HALO - d:POOL_HALO - d + rows, c0:c0 + POOL_GROUP_DIM]
        count = jnp.minimum(w, pos + 1).astype(F32)
        parts.append(_pool_branch(acc, token, count, wp_ref, ps_ref, g))
    o_b = jnp.concatenate(parts, axis=-1)
    x1_ref[...] = _ab_out(x_ref[...], ha_ref[...], o_b, gb_ref[...], wo_ref, g_ref, b_ref)
    uext_ref[0:POOL_HALO, :] = uext_ref[rows:rows + POOL_HALO, :]


def _ab_out_prompt(x2d, h_a, u_b, g_b, wp_bf, ps, wo_bf, ln_g, ln_b, batch, seq):
    rows = MIX_ROWS
    nt = seq // rows
    rmap = lambda b, t: (b * nt + t, 0)
    return pl.pallas_call(
        functools.partial(_ab_out_prompt_kernel, rows=rows),
        grid=(batch, nt),
        in_specs=[
            pl.BlockSpec((rows, D_MODEL), rmap),
            pl.BlockSpec((rows, SB_WIDTH), rmap),
            pl.BlockSpec((rows, POOL_WIDTH), rmap),
            pl.BlockSpec((rows, POOL_WIDTH), rmap),
            _const_spec(wp_bf.shape), _const_spec(ps.shape), _const_spec(wo_bf.shape),
            _const_spec(ln_g.shape), _const_spec(ln_b.shape),
        ],
        out_specs=pl.BlockSpec((rows, D_MODEL), rmap),
        out_shape=jax.ShapeDtypeStruct((batch * seq, D_MODEL), F32),
        scratch_shapes=[pltpu.VMEM((POOL_HALO + rows, POOL_WIDTH), F32)],
        compiler_params=pltpu.CompilerParams(
            dimension_semantics=("arbitrary", "arbitrary"), vmem_limit_bytes=VMEM_LIMIT),
        name="ab_out_prompt",
    )(x2d, h_a, u_b, g_b, wp_bf, ps, wo_bf, ln_g, ln_b)


def _glu_and_gate(x1, wi_ref):
    xb = x1.astype(BF16)
    a = _dot(xb, wi_ref[:, 0:CONV_WIDTH])
    a_gate = _dot(xb, wi_ref[:, CONV_WIDTH:2 * CONV_WIDTH])
    gate = _dot(xb, wi_ref[:, 2 * CONV_WIDTH:3 * CONV_WIDTH])
    return a * _sigmoid(a_gate), gate


def _c_out(x1, conv, gate, bdw_ref, cg_ref, cb_ref, wo_ref, g_ref, b_ref):
    c = _silu(_layer_norm(conv + bdw_ref[...], cg_ref[...], cb_ref[...]))
    op = _dot((c * _silu(gate)).astype(BF16), wo_ref[...])
    return _layer_norm(ALPHA * x1 + op, g_ref[...], b_ref[...])


def _cmix_prompt_kernel(x1_ref, wi_ref, wdw_ref, bdw_ref, cg_ref, cb_ref, wo_ref, g_ref, b_ref,
                        y_ref, htail_ref, hext_ref, conv_ref, shift_ref, *, rows):
    t = pl.program_id(1)

    @pl.when(t == 0)
    def _():
        hext_ref[0:CONV_HALO, :] = jnp.zeros((CONV_HALO, CONV_WIDTH), F32)

    x1 = x1_ref[...]
    h, gate = _glu_and_gate(x1, wi_ref)
    hext_ref[CONV_HALO:CONV_HALO + rows, :] = h
    span = rows + CONV_HALO - SUBLANES
    for s in range(1, SUBLANES):
        shift_ref[s - 1, 0:span, :] = hext_ref[s:s + span, :]
    base = CONV_HALO - CONV_STATE
    for cb in range(CONV_WIDTH // LANES):
        cols = slice(cb * LANES, (cb + 1) * LANES)
        taps = [wdw_ref[k:k + 1, cols] for k in range(CONV_KERNEL)]
        for rb in range(rows // CONV_ROW_CHUNK):
            r0 = rb * CONV_ROW_CHUNK
            acc = None
            for k in range(CONV_KERNEL):
                aligned, s = divmod(base + k, SUBLANES)
                lo_row = r0 + aligned * SUBLANES
                src = hext_ref if s == 0 else shift_ref.at[s - 1]
                term = src[lo_row:lo_row + CONV_ROW_CHUNK, cols] * taps[k]
                acc = term if acc is None else acc + term
            conv_ref[r0:r0 + CONV_ROW_CHUNK, cols] = acc
    y_ref[...] = _c_out(x1, conv_ref[...], gate, bdw_ref, cg_ref, cb_ref, wo_ref, g_ref, b_ref)
    hext_ref[0:CONV_HALO, :] = hext_ref[rows:rows + CONV_HALO, :]
    htail_ref[...] = hext_ref[rows:rows + CONV_HALO, :]


def _cmix_prompt(x1, wi_bf, w_dw, b_dw, cn_g, cn_b, wo_bf, ln_g, ln_b, batch, seq):
    rows = MIX_ROWS
    nt = seq // rows
    rmap = lambda b, t: (b * nt + t, 0)
    consts = [wi_bf, w_dw, b_dw, cn_g, cn_b, wo_bf, ln_g, ln_b]
    return pl.pallas_call(
        functools.partial(_cmix_prompt_kernel, rows=rows),
        grid=(batch, nt),
        in_specs=[pl.BlockSpec((rows, D_MODEL), rmap)] + [_const_spec(c.shape) for c in consts],
        out_specs=[pl.BlockSpec((rows, D_MODEL), rmap),
                   pl.BlockSpec((None, CONV_HALO, CONV_WIDTH), lambda b, t: (b, 0, 0))],
        out_shape=[jax.ShapeDtypeStruct((batch * seq, D_MODEL), F32),
                   jax.ShapeDtypeStruct((batch, CONV_HALO, CONV_WIDTH), F32)],
        scratch_shapes=[pltpu.VMEM((CONV_HALO + rows, CONV_WIDTH), F32),
                        pltpu.VMEM((rows, CONV_WIDTH), F32),
                        pltpu.VMEM((SUBLANES - 1, CONV_HALO + rows, CONV_WIDTH), F32)],
        compiler_params=pltpu.CompilerParams(
            dimension_semantics=("arbitrary", "arbitrary"), vmem_limit_bytes=VMEM_LIMIT),
        name="cmix_prompt",
    )(x1, *consts)


PAGE_ROWS = PAGE_SIZE * SB_HEADS


def _decode_kernel(pt_ref, bias_ref, q_ref, fold_ref, spread_ref, *refs, pages):
    del pt_ref
    k_refs = refs[:pages]
    v_refs = refs[pages:2 * pages]
    o_ref, carry_ref, acc_ref = refs[2 * pages:]
    step = pl.program_id(1)

    @pl.when(step == 0)
    def _():
        carry_ref[...] = jnp.zeros_like(carry_ref)
        acc_ref[...] = jnp.zeros_like(acc_ref)

    head = lax.broadcasted_iota(jnp.int32, (SB_HEADS, PAGE_ROWS), 0)
    row_head = lax.broadcasted_iota(jnp.int32, (SB_HEADS, PAGE_ROWS), 1) % SB_HEADS
    own = head == row_head
    page_rows = lambda x, p: x[p * SB_HEADS:(p + 1) * SB_HEADS]
    flat_bf16 = lambda ref: ref[...].reshape(PAGE_ROWS, SB_HEAD_DIM).astype(BF16)

    q = q_ref[...]
    y = [jnp.where(own, _dot_nt(q, flat_bf16(r)), 0.0) for r in k_refs]
    hi, lo = _split_bf16(jnp.concatenate(y, axis=0))
    z2 = _dot(jnp.concatenate([hi, lo], axis=0), fold_ref[...])
    rows = pages * SB_HEADS
    z = z2[0:rows] + z2[rows:2 * rows] + jnp.concatenate([bias_ref[...]] * pages, axis=0)
    sp = _softplus(z)
    ls = z - sp
    hi, lo = _split_bf16(sp)
    after2 = _dot(jnp.concatenate([hi, lo], axis=0), _strict_lower_ones(PAGE_SIZE))
    after = after2[0:rows] + after2[rows:2 * rows]
    carry = carry_ref[...]
    w_parts = [None] * pages
    for p in reversed(range(pages)):
        a_p = page_rows(after, p)
        w_parts[p] = jnp.exp(page_rows(ls, p) - a_p - carry)
        carry = carry + a_p[:, :1] + page_rows(sp, p)[:, :1]
    w_spread = _dot(jnp.concatenate(w_parts, axis=0).astype(BF16), spread_ref[...])
    acc = acc_ref[...]
    for p in range(pages):
        w_own = jnp.where(own, page_rows(w_spread, p), 0.0).astype(BF16)
        acc = acc + _dot(w_own, flat_bf16(v_refs[p]))
    acc_ref[...] = acc
    carry_ref[...] = carry

    @pl.when(step == pl.num_programs(1) - 1)
    def _():
        o_ref[...] = acc_ref[...]


def _decode_attention(page_table, bias_col, q_bf, cache_k, cache_v):
    nb, n_pages = page_table.shape
    pages = DECODE_PAGES
    steps = n_pages // pages
    key_of_row = jnp.arange(PAGE_ROWS, dtype=jnp.int32) // SB_HEADS
    fold = (key_of_row[:, None] == jnp.arange(PAGE_SIZE, dtype=jnp.int32)[None, :]).astype(BF16)
    spread = fold.T

    def page_map(slot):
        return lambda b, s, pt: (0, pt[b, n_pages - (s + 1) * pages + slot], 0, 0, 0)

    page_specs = [pl.BlockSpec((None, None, PAGE_SIZE, SB_HEADS, SB_HEAD_DIM), page_map(i)) for i in range(pages)]
    qmap = lambda b, s, pt: (b, 0, 0)
    const2 = lambda b, s, pt: (0, 0)
    grid_spec = pltpu.PrefetchScalarGridSpec(
        num_scalar_prefetch=1,
        grid=(nb, steps),
        in_specs=[pl.BlockSpec((SB_HEADS, 1), const2),
                  pl.BlockSpec((None, SB_HEADS, SB_HEAD_DIM), qmap),
                  pl.BlockSpec(fold.shape, const2),
                  pl.BlockSpec(spread.shape, const2)] + page_specs + page_specs,
        out_specs=pl.BlockSpec((None, SB_HEADS, SB_HEAD_DIM), qmap),
        scratch_shapes=[pltpu.VMEM((SB_HEADS, 1), F32), pltpu.VMEM((SB_HEADS, SB_HEAD_DIM), F32)],
    )
    return pl.pallas_call(
        functools.partial(_decode_kernel, pages=pages),
        grid_spec=grid_spec,
        out_shape=jax.ShapeDtypeStruct((nb, SB_HEADS, SB_HEAD_DIM), F32),
        compiler_params=pltpu.CompilerParams(
            dimension_semantics=("arbitrary", "arbitrary"), vmem_limit_bytes=VMEM_LIMIT),
        name="decode_attention",
    )(page_table, bias_col, q_bf.reshape(nb, SB_HEADS, SB_HEAD_DIM), fold, spread,
      *([cache_k] * pages), *([cache_v] * pages))


def _sample_tail_kernel(x_ref, oa_ref, ga_ref, ub_ref, gb_ref, pool_ref, conv_ref,
                        wp_ref, ps_ref, wo_ab_ref, g_ab_ref, b_ab_ref,
                        wi_ref, wdw_ref, bdw_ref, cg_ref, cb_ref, wo_c_ref, g_c_ref, b_c_ref,
                        y_ref, h_ref):
    ub = ub_ref[...]
    parts = []
    for g, w in enumerate(POOL_WINDOWS):
        c0 = g * POOL_GROUP_DIM
        token = ub[:, c0:c0 + POOL_GROUP_DIM]
        acc = token
        for d in range(1, w):
            acc = acc + pool_ref[POOL_STATE - d, :, c0:c0 + POOL_GROUP_DIM]
        parts.append(_pool_branch(acc, token, float(w), wp_ref, ps_ref, g))
    o_b = jnp.concatenate(parts, axis=-1)
    h_a = (oa_ref[...] * _silu(ga_ref[...])).astype(BF16)
    x1 = _ab_out(x_ref[...], h_a, o_b, gb_ref[...], wo_ab_ref, g_ab_ref, b_ab_ref)

    h, gate = _glu_and_gate(x1, wi_ref)
    h_ref[...] = h
    conv = h * wdw_ref[CONV_STATE:CONV_KERNEL, :]
    for k in range(CONV_STATE):
        conv = conv + conv_ref[k] * wdw_ref[k:k + 1, :]
    y_ref[...] = _c_out(x1, conv, gate, bdw_ref, cg_ref, cb_ref, wo_c_ref, g_c_ref, b_c_ref)


def _sample_tail(*args):
    nb = args[0].shape[0]
    return pl.pallas_call(
        _sample_tail_kernel,
        out_shape=[jax.ShapeDtypeStruct((nb, D_MODEL), F32), jax.ShapeDtypeStruct((nb, CONV_WIDTH), F32)],
        compiler_params=pltpu.CompilerParams(vmem_limit_bytes=VMEM_LIMIT),
        name="sample_tail",
    )(*args)


def kernel(x_prompt, x_sample, cache_k, cache_v, state_pool, state_conv, page_table, w_in_ab, sb_bias, w_pool,
           pool_scale, w_out_ab, ln_ab_g, ln_ab_b, w_in_c, w_dw, b_dw, conv_norm_g, conv_norm_b, w_out_c,
           ln_c_g, ln_c_b):
    assert DEPTH == 2 and w_in_ab.shape[0] == 1 and w_in_c.shape[0] == 1
    bp, seq, _ = x_prompt.shape
    bs, dec_seq, _ = x_sample.shape
    assert dec_seq == 1 and page_table.shape[1] * PAGE_SIZE >= max(POOL_WINDOWS)
    assert seq % ATTN_TILE == 0 and seq % MIX_ROWS == 0 and (bp * seq) % INPROJ_ROWS == 0

    w_in_ab_bf = w_in_ab[0].astype(BF16)
    w_out_ab_bf = w_out_ab[0].astype(BF16)
    w_pool_bf = w_pool[0].astype(BF16)
    w_in_c_bf = w_in_c[0].astype(BF16)
    w_out_c_bf = w_out_c[0].astype(BF16)
    ps = pool_scale[0].reshape(1, POOL_WIDTH)
    row = lambda a: a[0].reshape(1, -1)
    ln_ab = (row(ln_ab_g), row(ln_ab_b))
    c_params = (w_in_c_bf, w_dw[0], row(b_dw), row(conv_norm_g), row(conv_norm_b), w_out_c_bf,
                row(ln_c_g), row(ln_c_b))

    xp2d = x_prompt.reshape(bp * seq, D_MODEL)
    q_p, k_p, v_p, kb_p, vb_p, ga_p, ub_p, gb_p = _inproj_ab(xp2d, w_in_ab_bf, INPROJ_ROWS)
    ha_p = _prompt_attention(sb_bias[0], q_p, kb_p, vb_p, ga_p, bp, seq)
    x1_p = _ab_out_prompt(xp2d, ha_p, ub_p, gb_p, w_pool_bf, ps, w_out_ab_bf, *ln_ab, bp, seq)
    y_p, htail_p = _cmix_prompt(x1_p, *c_params, bp, seq)

    xs2d = x_sample.reshape(bs, D_MODEL)
    q_s, k_s, v_s, _, _, ga_s, ub_s, gb_s = _inproj_ab(xs2d, w_in_ab_bf, bs)
    oa_s = _decode_attention(page_table, sb_bias[0].reshape(SB_HEADS, 1), q_s, cache_k, cache_v)
    oa_s = oa_s.reshape(bs, SB_WIDTH)
    pool_tm = jnp.transpose(state_pool[0], (1, 0, 2))
    conv_tm = jnp.transpose(state_conv[0], (1, 0, 2))
    y_s, h_s = _sample_tail(xs2d, oa_s, ga_s, ub_s, gb_s, pool_tm, conv_tm,
                            w_pool_bf, ps, w_out_ab_bf, *ln_ab, *c_params)

    heads = (SB_HEADS, SB_HEAD_DIM)
    pool_prompt = ub_p.reshape(bp, seq, POOL_WIDTH)[:, seq - POOL_STATE:]
    pool_sample = jnp.concatenate([state_pool[0][:, 1:], ub_s[:, None, :]], axis=1)
    conv_prompt = htail_p[:, CONV_HALO - CONV_STATE:]
    conv_sample = jnp.concatenate([state_conv[0][:, 1:], h_s[:, None, :]], axis=1)
    return (y_p.reshape(bp, seq, D_MODEL), y_s.reshape(bs, 1, D_MODEL),
            k_p.reshape(1, bp, seq, *heads), v_p.reshape(1, bp, seq, *heads),
            k_s.reshape(1, bs, 1, *heads), v_s.reshape(1, bs, 1, *heads),
            pool_prompt[None], pool_sample[None], conv_prompt[None], conv_sample[None])
```

```python
import functools
import math

import jax
import jax.numpy as jnp
from jax import lax
from jax.experimental import pallas as pl
from jax.experimental.pallas import tpu as pltpu

F32 = jnp.float32
BF16 = jnp.bfloat16

D_MODEL = 1024
DEPTH = 2
SB_HEADS = 8
SB_HEAD_DIM = 64
SB_WIDTH = SB_HEADS * SB_HEAD_DIM
POOL_WIDTH = 512
POOL_WINDOWS = (2, 4, 8, 16)
POOL_GROUP_DIM = 128
POOL_STATE = 15
CONV_WIDTH = 1024
CONV_KERNEL = 31
CONV_STATE = CONV_KERNEL - 1
PAGE_SIZE = 128
LN_EPS = 1e-5
ALPHA = (2.0 * DEPTH) ** 0.25
Q_SCALE = 1.0 / math.sqrt(SB_HEAD_DIM)

LANES = 128
SUBLANES = 8
HEADS_PER_LANE_TILE = LANES // SB_HEAD_DIM
VMEM_LIMIT = 56 * 1024 * 1024

INPROJ_ROWS = 512
ATTN_TILE = 256
ATTN_LANE_TILES = 4
MIX_ROWS = 256
CONV_ROW_CHUNK = 64
POOL_HALO = 16
CONV_HALO = 32
DECODE_PAGES = 8
CUMSUM_CHUNK = 256


def _sigmoid(x):
    return 1.0 / (1.0 + jnp.exp(-x))


def _silu(x):
    return x * _sigmoid(x)


def _softplus(z):
    return jnp.maximum(z, 0.0) + jnp.log(1.0 + jnp.exp(-jnp.abs(z)))


def _layer_norm(x, g, b):
    mu = jnp.mean(x, axis=-1, keepdims=True)
    xc = x - mu
    var = jnp.mean(xc * xc, axis=-1, keepdims=True)
    return xc * lax.rsqrt(var + LN_EPS) * g + b


def _dot(a, b):
    return jnp.dot(a, b, preferred_element_type=F32)


def _dot_nt(a, b):
    return lax.dot_general(a, b, (((1,), (1,)), ((), ())), preferred_element_type=F32)


def _split_bf16(x):
    hi = x.astype(BF16)
    lo = (x - hi.astype(F32)).astype(BF16)
    return hi, lo


def _strict_lower_ones(n):
    r = lax.broadcasted_iota(jnp.int32, (n, n), 0)
    c = lax.broadcasted_iota(jnp.int32, (n, n), 1)
    return jnp.where(r > c, 1.0, 0.0).astype(BF16)


def _const_spec(shape):
    zeros = (0,) * len(shape)
    return pl.BlockSpec(shape, lambda *_: zeros)


def _inproj_ab_kernel(x_ref, w_ref, q_ref, k_ref, v_ref, kb_ref, vb_ref, ga_ref, ub_ref, gb_ref):
    xb = x_ref[...].astype(BF16)

    def proj(c):
        return _dot(xb, w_ref[:, c * SB_WIDTH:(c + 1) * SB_WIDTH])

    q_ref[...] = (proj(0) * Q_SCALE).astype(BF16)
    k = proj(1)
    k_ref[...] = k
    kb_ref[...] = k.astype(BF16)
    v = proj(2)
    v_ref[...] = v
    vb_ref[...] = v.astype(BF16)
    ga_ref[...] = proj(3)
    ub_ref[...] = proj(4)
    gb_ref[...] = proj(5)


def _inproj_ab(x2d, w_bf, rows):
    m = x2d.shape[0]
    col = lambda i: (i, 0)
    wide = lambda dt: jax.ShapeDtypeStruct((m, SB_WIDTH), dt)
    out_block = pl.BlockSpec((rows, SB_WIDTH), col)
    return pl.pallas_call(
        _inproj_ab_kernel,
        grid=(m // rows,),
        in_specs=[pl.BlockSpec((rows, D_MODEL), col), _const_spec(w_bf.shape)],
        out_specs=[out_block] * 8,
        out_shape=[wide(BF16), wide(F32), wide(F32), wide(BF16), wide(BF16), wide(F32), wide(F32), wide(F32)],
        compiler_params=pltpu.CompilerParams(dimension_semantics=("arbitrary",), vmem_limit_bytes=VMEM_LIMIT),
        name="inproj_ab",
    )(x2d, w_bf)


def _sb_tile(qh, kt, vt, bias, upper, o_acc, c_acc, causal):
    z = _dot_nt(qh, kt) + bias
    sp = _softplus(z)
    ls = z - sp
    if causal is not None:
        sp = jnp.where(causal, sp, 0.0)
    hi, lo = _split_bf16(sp)
    after = _dot(hi, upper) + _dot(lo, upper)
    w = jnp.exp(ls - after - c_acc)
    if causal is not None:
        w = jnp.where(causal, w, 0.0)
    o_acc = o_acc + _dot(w.astype(BF16), vt)
    c_acc = c_acc + after[:, :1] + sp[:, :1]
    return o_acc, c_acc


def _attn_kernel(bias_ref, q_ref, k_ref, v_ref, g_ref, o_ref, *, tile, lane_tiles):
    group = pl.program_id(1)
    qi = pl.program_id(2)
    nheads = lane_tiles * HEADS_PER_LANE_TILE
    lane = lax.broadcasted_iota(jnp.int32, (1, LANES), 1)
    q_heads, biases = [], []
    for h in range(nheads):
        lt, sub = divmod(h, HEADS_PER_LANE_TILE)
        q = q_ref[:, lt * LANES:(lt + 1) * LANES].astype(F32)
        q_heads.append(jnp.where((lane // SB_HEAD_DIM) == sub, q, 0.0).astype(BF16))
        biases.append(bias_ref[nheads * group + h])
    upper = _strict_lower_ones(tile)
    row = lax.broadcasted_iota(jnp.int32, (tile, tile), 0)
    col = lax.broadcasted_iota(jnp.int32, (tile, tile), 1)
    causal = col < row

    def visit(start, carry, mask):
        hs = range(nheads)
        tiles = lambda ref: [ref[pl.ds(start, tile), lt * LANES:(lt + 1) * LANES] for lt in range(lane_tiles)]
        kts, vts = tiles(k_ref), tiles(v_ref)
        z = [_dot_nt(q_heads[h], kts[h // HEADS_PER_LANE_TILE]) + biases[h] for h in hs]
        sp = [_softplus(z[h]) for h in hs]
        ls = [z[h] - sp[h] for h in hs]
        if mask is not None:
            sp = [jnp.where(mask, s, 0.0) for s in sp]
        split = [_split_bf16(s) for s in sp]
        after = [_dot(split[h][0], upper) + _dot(split[h][1], upper) for h in hs]
        w = [jnp.exp(ls[h] - after[h] - carry[h][1]) for h in hs]
        if mask is not None:
            w = [jnp.where(mask, x, 0.0) for x in w]
        o = [carry[h][0] + _dot(w[h].astype(BF16), vts[h // HEADS_PER_LANE_TILE]) for h in hs]
        c = [carry[h][1] + after[h][:, :1] + sp[h][:, :1] for h in hs]
        return tuple((o[h], c[h]) for h in hs)

    init = tuple((jnp.zeros((tile, LANES), F32), jnp.zeros((tile, 1), F32)) for _ in range(nheads))
    carry = visit(pl.multiple_of(qi * tile, tile), init, causal)

    def body(step, carry):
        start = pl.multiple_of((qi - 1 - step) * tile, tile)
        return visit(start, carry, None)

    carry = lax.fori_loop(0, qi, body, carry)
    for lt in range(lane_tiles):
        h0 = lt * HEADS_PER_LANE_TILE
        o = jnp.where((lane // SB_HEAD_DIM) == 0, carry[h0][0], carry[h0 + 1][0])
        cols = slice(lt * LANES, (lt + 1) * LANES)
        o_ref[:, cols] = (o * _silu(g_ref[:, cols])).astype(BF16)


def _prompt_attention(sb_bias, q_bf, k_bf, v_bf, g_a, batch, seq):
    tile = ATTN_TILE
    nq = seq // tile
    width = ATTN_LANE_TILES * LANES
    groups = SB_WIDTH // width
    qmap = lambda b, p, i: (b * nq + i, p)
    kvmap = lambda b, p, i: (b, p)
    return pl.pallas_call(
        functools.partial(_attn_kernel, tile=tile, lane_tiles=ATTN_LANE_TILES),
        grid=(batch, groups, nq),
        in_specs=[
            pl.BlockSpec(memory_space=pltpu.SMEM),
            pl.BlockSpec((tile, width), qmap),
            pl.BlockSpec((seq, width), kvmap),
            pl.BlockSpec((seq, width), kvmap),
            pl.BlockSpec((tile, width), qmap),
        ],
        out_specs=pl.BlockSpec((tile, width), qmap),
        out_shape=jax.ShapeDtypeStruct((batch * seq, SB_WIDTH), BF16),
        compiler_params=pltpu.CompilerParams(
            dimension_semantics=("arbitrary", "arbitrary", "arbitrary"), vmem_limit_bytes=VMEM_LIMIT),
        name="prompt_attention",
    )(sb_bias, q_bf, k_bf, v_bf, g_a)


def _pool_branch(window_sum, token, count, wp_ref, ps_ref, group):
    c0 = group * POOL_GROUP_DIM
    d = window_sum / count - token
    return _dot(d.astype(BF16), wp_ref[group]) * ps_ref[:, c0:c0 + POOL_GROUP_DIM]


def _ab_out(x, h_a_bf, o_b, g_b, wo_ref, g_ref, b_ref):
    h_b = (o_b * _silu(g_b)).astype(BF16)
    op = _dot(h_a_bf, wo_ref[0:SB_WIDTH, :]) + _dot(h_b, wo_ref[SB_WIDTH:SB_WIDTH + POOL_WIDTH, :])
    return _layer_norm(ALPHA * x + op, g_ref[...], b_ref[...])


def _ab_out_prompt_kernel(x_ref, ha_ref, ub_ref, gb_ref, wp_ref, ps_ref, wo_ref, g_ref, b_ref,
                          x1_ref, uext_ref, *, rows):
    t = pl.program_id(1)

    @pl.when(t == 0)
    def _():
        uext_ref[0:POOL_HALO, :] = jnp.zeros((POOL_HALO, POOL_WIDTH), F32)

    ub = ub_ref[...]
    uext_ref[POOL_HALO:POOL_HALO + rows, :] = ub
    pos = lax.broadcasted_iota(jnp.int32, (rows, 1), 0) + t * rows
    parts = []
    for g, w in enumerate(POOL_WINDOWS):
        c0 = g * POOL_GROUP_DIM
        token = ub[:, c0:c0 + POOL_GROUP_DIM]
        acc = token
        for d in range(1, w):
            acc = acc + uext_ref[POOL_HALO - d:POOL_HALO - d + rows, c0:c0 + POOL_GROUP_DIM]
        count = jnp.minimum(w, pos + 1).astype(F32)
        parts.append(_pool_branch(acc, token, count, wp_ref, ps_ref, g))
    o_b = jnp.concatenate(parts, axis=-1)
    x1_ref[...] = _ab_out(x_ref[...], ha_ref[...], o_b, gb_ref[...], wo_ref, g_ref, b_ref)
    uext_ref[0:POOL_HALO, :] = uext_ref[rows:rows + POOL_HALO, :]


def _ab_out_prompt(x2d, h_a, u_b, g_b, wp_bf, ps, wo_bf, ln_g, ln_b, batch, seq):
    rows = MIX_ROWS
    nt = seq // rows
    rmap = lambda b, t: (b * nt + t, 0)
    return pl.pallas_call(
        functools.partial(_ab_out_prompt_kernel, rows=rows),
        grid=(batch, nt),
        in_specs=[
            pl.BlockSpec((rows, D_MODEL), rmap),
            pl.BlockSpec((rows, SB_WIDTH), rmap),
            pl.BlockSpec((rows, POOL_WIDTH), rmap),
            pl.BlockSpec((rows, POOL_WIDTH), rmap),
            _const_spec(wp_bf.shape), _const_spec(ps.shape), _const_spec(wo_bf.shape),
            _const_spec(ln_g.shape), _const_spec(ln_b.shape),
        ],
        out_specs=pl.BlockSpec((rows, D_MODEL), rmap),
        out_shape=jax.ShapeDtypeStruct((batch * seq, D_MODEL), F32),
        scratch_shapes=[pltpu.VMEM((POOL_HALO + rows, POOL_WIDTH), F32)],
        compiler_params=pltpu.CompilerParams(
            dimension_semantics=("arbitrary", "arbitrary"), vmem_limit_bytes=VMEM_LIMIT),
        name="ab_out_prompt",
    )(x2d, h_a, u_b, g_b, wp_bf, ps, wo_bf, ln_g, ln_b)


def _glu_and_gate(x1, wi_ref):
    xb = x1.astype(BF16)
    a = _dot(xb, wi_ref[:, 0:CONV_WIDTH])
    a_gate = _dot(xb, wi_ref[:, CONV_WIDTH:2 * CONV_WIDTH])
    gate = _dot(xb, wi_ref[:, 2 * CONV_WIDTH:3 * CONV_WIDTH])
    return a * _sigmoid(a_gate), gate


def _c_out(x1, conv, gate, bdw_ref, cg_ref, cb_ref, wo_ref, g_ref, b_ref):
    c = _silu(_layer_norm(conv + bdw_ref[...], cg_ref[...], cb_ref[...]))
    op = _dot((c * _silu(gate)).astype(BF16), wo_ref[...])
    return _layer_norm(ALPHA * x1 + op, g_ref[...], b_ref[...])


def _cmix_prompt_kernel(x1_ref, wi_ref, wdw_ref, bdw_ref, cg_ref, cb_ref, wo_ref, g_ref, b_ref,
                        y_ref, htail_ref, hext_ref, conv_ref, shift_ref, *, rows):
    t = pl.program_id(1)

    @pl.when(t == 0)
    def _():
        hext_ref[0:CONV_HALO, :] = jnp.zeros((CONV_HALO, CONV_WIDTH), F32)

    x1 = x1_ref[...]
    h, gate = _glu_and_gate(x1, wi_ref)
    hext_ref[CONV_HALO:CONV_HALO + rows, :] = h
    span = rows + CONV_HALO - SUBLANES
    for s in range(1, SUBLANES):
        shift_ref[s - 1, 0:span, :] = hext_ref[s:s + span, :]
    base = CONV_HALO - CONV_STATE
    for cb in range(CONV_WIDTH // LANES):
        cols = slice(cb * LANES, (cb + 1) * LANES)
        taps = [wdw_ref[k:k + 1, cols] for k in range(CONV_KERNEL)]
        for rb in range(rows // CONV_ROW_CHUNK):
            r0 = rb * CONV_ROW_CHUNK
            acc = None
            for k in range(CONV_KERNEL):
                aligned, s = divmod(base + k, SUBLANES)
                lo_row = r0 + aligned * SUBLANES
                src = hext_ref if s == 0 else shift_ref.at[s - 1]
                term = src[lo_row:lo_row + CONV_ROW_CHUNK, cols] * taps[k]
                acc = term if acc is None else acc + term
            conv_ref[r0:r0 + CONV_ROW_CHUNK, cols] = acc
    y_ref[...] = _c_out(x1, conv_ref[...], gate, bdw_ref, cg_ref, cb_ref, wo_ref, g_ref, b_ref)
    hext_ref[0:CONV_HALO, :] = hext_ref[rows:rows + CONV_HALO, :]
    htail_ref[...] = hext_ref[rows:rows + CONV_HALO, :]


def _cmix_prompt(x1, wi_bf, w_dw, b_dw, cn_g, cn_b, wo_bf, ln_g, ln_b, batch, seq):
    rows = MIX_ROWS
    nt = seq // rows
    rmap = lambda b, t: (b * nt + t, 0)
    consts = [wi_bf, w_dw, b_dw, cn_g, cn_b, wo_bf, ln_g, ln_b]
    return pl.pallas_call(
        functools.partial(_cmix_prompt_kernel, rows=rows),
        grid=(batch, nt),
        in_specs=[pl.BlockSpec((rows, D_MODEL), rmap)] + [_const_spec(c.shape) for c in consts],
        out_specs=[pl.BlockSpec((rows, D_MODEL), rmap),
                   pl.BlockSpec((None, CONV_HALO, CONV_WIDTH), lambda b, t: (b, 0, 0))],
        out_shape=[jax.ShapeDtypeStruct((batch * seq, D_MODEL), F32),
                   jax.ShapeDtypeStruct((batch, CONV_HALO, CONV_WIDTH), F32)],
        scratch_shapes=[pltpu.VMEM((CONV_HALO + rows, CONV_WIDTH), F32),
                        pltpu.VMEM((rows, CONV_WIDTH), F32),
                        pltpu.VMEM((SUBLANES - 1, CONV_HALO + rows, CONV_WIDTH), F32)],
        compiler_params=pltpu.CompilerParams(
            dimension_semantics=("arbitrary", "arbitrary"), vmem_limit_bytes=VMEM_LIMIT),
        name="cmix_prompt",
    )(x1, *consts)


def _decode_kernel(pt_ref, bias_ref, q_ref, *refs, pages):
    del pt_ref
    k_refs = refs[:pages]
    v_refs = refs[pages:2 * pages]
    o_ref, carry_ref, acc_ref = refs[2 * pages:]
    step = pl.program_id(1)

    @pl.when(step == 0)
    def _():
        carry_ref[...] = jnp.zeros_like(carry_ref)
        acc_ref[...] = jnp.zeros_like(acc_ref)

    head = lax.broadcasted_iota(jnp.int32, (SB_HEADS, SB_WIDTH), 0)
    owner = lax.broadcasted_iota(jnp.int32, (SB_HEADS, SB_WIDTH), 1) // SB_HEAD_DIM
    own = head == owner
    q = jnp.broadcast_to(q_ref[...].astype(F32), (SB_HEADS, SB_WIDTH))
    q_heads = jnp.where(own, q, 0.0).astype(BF16)
    kb = jnp.concatenate([r[...] for r in k_refs], axis=0).astype(BF16)
    vb = jnp.concatenate([r[...] for r in v_refs], axis=0).astype(BF16)
    z = _dot_nt(q_heads, kb) + bias_ref[...]
    sp = _softplus(z)
    ls = z - sp
    nchunk = pages * PAGE_SIZE // CUMSUM_CHUNK
    chunk = lambda x, c: x[:, c * CUMSUM_CHUNK:(c + 1) * CUMSUM_CHUNK]
    sp_stack = jnp.concatenate([chunk(sp, c) for c in range(nchunk)], axis=0)
    hi, lo = _split_bf16(sp_stack)
    after2 = _dot(jnp.concatenate([hi, lo], axis=0), _strict_lower_ones(CUMSUM_CHUNK))
    rows = nchunk * SB_HEADS
    after = after2[0:rows] + after2[rows:2 * rows]
    carry = carry_ref[...]
    w_parts = [None] * nchunk
    for c in reversed(range(nchunk)):
        a_c = after[c * SB_HEADS:(c + 1) * SB_HEADS]
        sp_c = chunk(sp, c)
        w_parts[c] = jnp.exp(chunk(ls, c) - a_c - carry)
        carry = carry + a_c[:, :1] + sp_c[:, :1]
    w = jnp.concatenate(w_parts, axis=1).astype(BF16)
    acc_ref[...] += _dot(w, vb)
    carry_ref[...] = carry

    @pl.when(step == pl.num_programs(1) - 1)
    def _():
        o_ref[...] = jnp.sum(jnp.where(own, acc_ref[...], 0.0), axis=0, keepdims=True)


def _decode_attention(page_table, bias_col, q_bf, cache_k, cache_v):
    nb, n_pages = page_table.shape
    pages = DECODE_PAGES
    steps = n_pages // pages

    def page_map(slot):
        return lambda b, s, pt: (pt[b, n_pages - (s + 1) * pages + slot], 0, 0)

    page_specs = [pl.BlockSpec((None, PAGE_SIZE, SB_WIDTH), page_map(i)) for i in range(pages)]
    qmap = lambda b, s, pt: (b, 0, 0)
    grid_spec = pltpu.PrefetchScalarGridSpec(
        num_scalar_prefetch=1,
        grid=(nb, steps),
        in_specs=[pl.BlockSpec((SB_HEADS, 1), lambda b, s, pt: (0, 0)),
                  pl.BlockSpec((None, 1, SB_WIDTH), qmap)] + page_specs + page_specs,
        out_specs=pl.BlockSpec((None, 1, SB_WIDTH), qmap),
        scratch_shapes=[pltpu.VMEM((SB_HEADS, 1), F32), pltpu.VMEM((SB_HEADS, SB_WIDTH), F32)],
    )
    return pl.pallas_call(
        functools.partial(_decode_kernel, pages=pages),
        grid_spec=grid_spec,
        out_shape=jax.ShapeDtypeStruct((nb, 1, SB_WIDTH), F32),
        compiler_params=pltpu.CompilerParams(
            dimension_semantics=("arbitrary", "arbitrary"), vmem_limit_bytes=VMEM_LIMIT),
        name="decode_attention",
    )(page_table, bias_col, q_bf.reshape(nb, 1, SB_WIDTH), *([cache_k] * pages), *([cache_v] * pages))


def _sample_tail_kernel(x_ref, oa_ref, ga_ref, ub_ref, gb_ref, pool_ref, conv_ref,
                        wp_ref, ps_ref, wo_ab_ref, g_ab_ref, b_ab_ref,
                        wi_ref, wdw_ref, bdw_ref, cg_ref, cb_ref, wo_c_ref, g_c_ref, b_c_ref,
                        y_ref, h_ref):
    ub = ub_ref[...]
    parts = []
    for g, w in enumerate(POOL_WINDOWS):
        c0 = g * POOL_GROUP_DIM
        token = ub[:, c0:c0 + POOL_GROUP_DIM]
        acc = token
        for d in range(1, w):
            acc = acc + pool_ref[POOL_STATE - d, :, c0:c0 + POOL_GROUP_DIM]
        parts.append(_pool_branch(acc, token, float(w), wp_ref, ps_ref, g))
    o_b = jnp.concatenate(parts, axis=-1)
    h_a = (oa_ref[...] * _silu(ga_ref[...])).astype(BF16)
    x1 = _ab_out(x_ref[...], h_a, o_b, gb_ref[...], wo_ab_ref, g_ab_ref, b_ab_ref)

    h, gate = _glu_and_gate(x1, wi_ref)
    h_ref[...] = h
    conv = h * wdw_ref[CONV_STATE:CONV_KERNEL, :]
    for k in range(CONV_STATE):
        conv = conv + conv_ref[k] * wdw_ref[k:k + 1, :]
    y_ref[...] = _c_out(x1, conv, gate, bdw_ref, cg_ref, cb_ref, wo_c_ref, g_c_ref, b_c_ref)


def _sample_tail(*args):
    nb = args[0].shape[0]
    return pl.pallas_call(
        _sample_tail_kernel,
        out_shape=[jax.ShapeDtypeStruct((nb, D_MODEL), F32), jax.ShapeDtypeStruct((nb, CONV_WIDTH), F32)],
        compiler_params=pltpu.CompilerParams(vmem_limit_bytes=VMEM_LIMIT),
        name="sample_tail",
    )(*args)


def kernel(x_prompt, x_sample, cache_k, cache_v, state_pool, state_conv, page_table, w_in_ab, sb_bias, w_pool,
           pool_scale, w_out_ab, ln_ab_g, ln_ab_b, w_in_c, w_dw, b_dw, conv_norm_g, conv_norm_b, w_out_c,
           ln_c_g, ln_c_b):
    assert DEPTH == 2 and w_in_ab.shape[0] == 1 and w_in_c.shape[0] == 1
    bp, seq, _ = x_prompt.shape
    bs, dec_seq, _ = x_sample.shape
    assert dec_seq == 1 and page_table.shape[1] * PAGE_SIZE >= max(POOL_WINDOWS)
    assert seq % ATTN_TILE == 0 and seq % MIX_ROWS == 0 and (bp * seq) % INPROJ_ROWS == 0

    w_in_ab_bf = w_in_ab[0].astype(BF16)
    w_out_ab_bf = w_out_ab[0].astype(BF16)
    w_pool_bf = w_pool[0].astype(BF16)
    w_in_c_bf = w_in_c[0].astype(BF16)
    w_out_c_bf = w_out_c[0].astype(BF16)
    ps = pool_scale[0].reshape(1, POOL_WIDTH)
    row = lambda a: a[0].reshape(1, -1)
    ln_ab = (row(ln_ab_g), row(ln_ab_b))
    c_params = (w_in_c_bf, w_dw[0], row(b_dw), row(conv_norm_g), row(conv_norm_b), w_out_c_bf,
                row(ln_c_g), row(ln_c_b))

    xp2d = x_prompt.reshape(bp * seq, D_MODEL)
    q_p, k_p, v_p, kb_p, vb_p, ga_p, ub_p, gb_p = _inproj_ab(xp2d, w_in_ab_bf, INPROJ_ROWS)
    ha_p = _prompt_attention(sb_bias[0], q_p, kb_p, vb_p, ga_p, bp, seq)
    x1_p = _ab_out_prompt(xp2d, ha_p, ub_p, gb_p, w_pool_bf, ps, w_out_ab_bf, *ln_ab, bp, seq)
    y_p, htail_p = _cmix_prompt(x1_p, *c_params, bp, seq)

    xs2d = x_sample.reshape(bs, D_MODEL)
    q_s, k_s, v_s, _, _, ga_s, ub_s, gb_s = _inproj_ab(xs2d, w_in_ab_bf, bs)
    n_pool = cache_k.shape[1]
    ck = cache_k[0].reshape(n_pool, PAGE_SIZE, SB_WIDTH)
    cv = cache_v[0].reshape(n_pool, PAGE_SIZE, SB_WIDTH)
    oa_s = _decode_attention(page_table, sb_bias[0].reshape(SB_HEADS, 1), q_s, ck, cv).reshape(bs, SB_WIDTH)
    pool_tm = jnp.transpose(state_pool[0], (1, 0, 2))
    conv_tm = jnp.transpose(state_conv[0], (1, 0, 2))
    y_s, h_s = _sample_tail(xs2d, oa_s, ga_s, ub_s, gb_s, pool_tm, conv_tm,
                            w_pool_bf, ps, w_out_ab_bf, *ln_ab, *c_params)

    heads = (SB_HEADS, SB_HEAD_DIM)
    pool_prompt = ub_p.reshape(bp, seq, POOL_WIDTH)[:, seq - POOL_STATE:]
    pool_sample = jnp.concatenate([state_pool[0][:, 1:], ub_s[:, None, :]], axis=1)
    conv_prompt = htail_p[:, CONV_HALO - CONV_STATE:]
    conv_sample = jnp.concatenate([state_conv[0][:, 1:], h_s[:, None, :]], axis=1)
    return (y_p.reshape(bp, seq, D_MODEL), y_s.reshape(bs, 1, D_MODEL),
            k_p.reshape(1, bp, seq, *heads), v_p.reshape(1, bp, seq, *heads),
            k_s.reshape(1, bs, 1, *heads), v_s.reshape(1, bs, 1, *heads),
            pool_prompt[None], pool_sample[None], conv_prompt[None], conv_sample[None])
```
